```python
import jax, jax.numpy as jnp
from jax import lax
import numpy as np

D_MODEL = 1024
BATCH = 2
SEQ = 8192
DEPTH = 2

HEAD_DIM = 64
ATT_HEADS = (D_MODEL // 2) // HEAD_DIM
ATT_WIDTH = ATT_HEADS * HEAD_DIM
DILATED_BRANCHES = ((128, 1), (512, 4), (2048, 16))
ATT_BLOCK = 128
CONV_WIDTH = D_MODEL // 4
CONV_K = 3
GLA_HEADS = 4
GLA_VAL_WIDTH = D_MODEL // 4
GLA_KEY_WIDTH = GLA_VAL_WIDTH // 2
GLA_DV = GLA_VAL_WIDTH // GLA_HEADS
GLA_DK = GLA_KEY_WIDTH // GLA_HEADS
GLA_GATE_RANK = 16
GLA_GATE_TAU = 16.0
GLA_CHUNK = 32
MIX_WIDTH = ATT_WIDTH + CONV_WIDTH + GLA_VAL_WIDTH
IN_WIDTHS = (ATT_WIDTH, ATT_WIDTH, ATT_WIDTH,
             CONV_WIDTH, CONV_WIDTH, CONV_WIDTH,
             GLA_KEY_WIDTH, GLA_KEY_WIDTH,
             GLA_VAL_WIDTH, GLA_VAL_WIDTH,
             GLA_GATE_RANK)
IN_TOTAL = sum(IN_WIDTHS)
D_FF_DENSE = 2816
N_EXPERTS = 8
TOP_K = 2
D_FF_EXPERT = 3584
ROPE_THETA = 10000.0
EPS = 1e-6
NEG = -1e30

kernel_name = "hybrid_dilated_conv_gla_moe_trunk"


def rmsnorm(x, g):
    xf = x.astype(jnp.float32)
    y = xf * lax.rsqrt(jnp.mean(xf * xf, axis=-1, keepdims=True) + EPS)
    return (y * g.astype(jnp.float32)).astype(x.dtype)


def rope_tables(seq, dim):
    inv = 1.0 / (ROPE_THETA ** (jnp.arange(0, dim, 2, dtype=jnp.float32) / dim))
    ang = jnp.arange(seq, dtype=jnp.float32)[:, None] * inv[None, :]
    return jnp.cos(ang), jnp.sin(ang)


def apply_rope(x, cos, sin):
    x1, x2 = jnp.split(x.astype(jnp.float32), 2, axis=-1)
    c = cos[None, :, None, :]
    s = sin[None, :, None, :]
    return jnp.concatenate([x1 * c - x2 * s, x1 * s + x2 * c], axis=-1)


def dilated_branch(q, k, v, window, dilation):
    b, s, h, d = q.shape
    n_back = window // dilation
    assert n_back <= ATT_BLOCK
    L = s // dilation
    nb = -(-L // ATT_BLOCK)
    Lp = nb * ATT_BLOCK

    def to_strided(t):
        t = t.reshape(b, L, dilation, h, d).transpose(0, 2, 3, 1, 4)
        t = jnp.pad(t, ((0, 0), (0, 0), (0, 0), (0, Lp - L), (0, 0)))
        return t.reshape(b, dilation, h, nb, ATT_BLOCK, d)

    def with_prev(t):
        prev = jnp.pad(t, ((0, 0), (0, 0), (0, 0), (1, 0), (0, 0), (0, 0)))[:, :, :, :-1]
        return jnp.concatenate([prev, t], axis=4)

    qs = to_strided(q)
    kk = with_prev(to_strided(k))
    vv = with_prev(to_strided(v))
    scores = jnp.einsum('brhnqd,brhnkd->brhnqk', qs, kk) * (d ** -0.5)
    blk = jnp.arange(nb)[:, None, None]
    qi = blk * ATT_BLOCK + jnp.arange(ATT_BLOCK)[None, :, None]
    ki = (blk - 1) * ATT_BLOCK + jnp.arange(2 * ATT_BLOCK)[None, None, :]
    diff = qi - ki
    valid = (diff >= 0) & (diff <= n_back) & (ki >= 0)
    scores = jnp.where(valid, scores, NEG)
    mx = jnp.max(scores, axis=-1)
    p = jnp.where(valid, jnp.exp(scores - mx[..., None]), 0.0)
    den = jnp.sum(p, axis=-1)
    num = jnp.einsum('brhnqk,brhnkd->brhnqd', p, vv)

    def from_strided(t):
        tail = t.shape[5:]
        t = t.reshape((b, dilation, h, Lp) + tail)[:, :, :, :L]
        perm = (0, 3, 1, 2) + tuple(range(4, t.ndim))
        return t.transpose(perm).reshape((b, s, h) + tail)

    return from_strided(num), from_strided(den), from_strided(mx)


def dilated_attention(q, k, v):
    outs = [dilated_branch(q, k, v, w, dl) for (w, dl) in DILATED_BRANCHES]
    m_all = jnp.max(jnp.stack([o[2] for o in outs]), axis=0)
    num = 0.0
    den = 0.0
    for n_i, d_i, m_i in outs:
        w_i = jnp.exp(m_i - m_all)
        num = num + w_i[..., None] * n_i
        den = den + w_i * d_i
    return num / den[..., None]


def short_gated_conv(gate_b, gate_c, hin, conv_w):
    u = gate_c * hin
    y = lax.conv_general_dilated(u, conv_w[:, None, :].astype(u.dtype), window_strides=(1,),
                                 padding=[(CONV_K - 1, 0)],
                                 dimension_numbers=('NWC', 'WIO', 'NWC'),
                                 feature_group_count=u.shape[-1])
    return gate_b * y


def gla(q, k, v, log_a, out_gate, norm_g):
    f32 = jnp.float32
    b, s, h, dk = q.shape
    dv = v.shape[-1]
    C = GLA_CHUNK
    n = s // C

    def chunk(t):
        return t.astype(f32).reshape(b, n, C, h, -1).transpose(0, 3, 1, 2, 4)

    qc = chunk(q) * (dk ** -0.5)
    kc, vc, ac = chunk(k), chunk(v), chunk(log_a)
    bcum = jnp.cumsum(ac, axis=3)
    causal = jnp.tril(jnp.ones((C, C), dtype=bool))[..., None]
    diff = bcum[:, :, :, :, None, :] - bcum[:, :, :, None, :, :]
    decay = jnp.where(causal, jnp.exp(jnp.minimum(diff, 0.0)), 0.0)
    attn = jnp.einsum('bhntk,bhnsk,bhntsk->bhnts', qc, kc, decay)
    o_intra = jnp.einsum('bhnts,bhnsv->bhntv', attn, vc)
    b_last = bcum[:, :, :, -1, :]
    k_to_end = kc * jnp.exp(b_last[:, :, :, None, :] - bcum)
    chunk_state = jnp.einsum('bhnsk,bhnsv->bhnkv', k_to_end, vc)

    def step(state, inp):
        a_last, upd = inp
        return jnp.exp(a_last)[..., None] * state + upd, state

    init = jnp.zeros((b, h, dk, dv), f32)
    _, s_prev = lax.scan(step, init, (jnp.moveaxis(b_last, 2, 0), jnp.moveaxis(chunk_state, 2, 0)))
    s_prev = jnp.moveaxis(s_prev, 0, 2)
    o_inter = jnp.einsum('bhntk,bhnkv->bhntv', qc * jnp.exp(bcum), s_prev)
    o = (o_intra + o_inter).transpose(0, 2, 3, 1, 4).reshape(b, s, h, dv)
    o = o * lax.rsqrt(jnp.mean(o * o, axis=-1, keepdims=True) + EPS)
    return o.reshape(b, s, h * dv) * norm_g.astype(f32) * jax.nn.silu(out_gate.astype(f32))


def hybrid_mixer(hn, w_in, w_gate_lr, b_gate_lr, conv_w, gla_norm, w_out, cos, sin):
    b, s, _ = hn.shape
    split_pts = [int(p) for p in np.cumsum(IN_WIDTHS)[:-1]]
    proj = hn @ w_in
    (q_a, k_a, v_a, c_b, c_c, c_h, g_q, g_k, g_v, g_o, g_z) = jnp.split(proj, split_pts, axis=-1)
    qa = apply_rope(q_a.reshape(b, s, ATT_HEADS, HEAD_DIM), cos, sin)
    ka = apply_rope(k_a.reshape(b, s, ATT_HEADS, HEAD_DIM), cos, sin)
    va = v_a.reshape(b, s, ATT_HEADS, HEAD_DIM).astype(jnp.float32)
    att = dilated_attention(qa, ka, va).reshape(b, s, ATT_WIDTH)
    cv = short_gated_conv(c_b, c_c, c_h, conv_w)
    log_a = jax.nn.log_sigmoid((g_z @ w_gate_lr + b_gate_lr).astype(jnp.float32)) / GLA_GATE_TAU
    gl = gla(g_q.reshape(b, s, GLA_HEADS, GLA_DK), g_k.reshape(b, s, GLA_HEADS, GLA_DK),
             g_v.reshape(b, s, GLA_HEADS, GLA_DV), log_a.reshape(b, s, GLA_HEADS, GLA_DK),
             g_o, gla_norm)
    mix = jnp.concatenate([att.astype(hn.dtype), cv.astype(hn.dtype), gl.astype(hn.dtype)], axis=-1)
    return mix @ w_out


def swiglu(h, wg, wu, wd):
    return (jax.nn.silu(h @ wg) * (h @ wu)) @ wd


def moe_swiglu(h, router, eg, eu, ed):
    b, s, d = h.shape
    t = h.reshape(b * s, d)
    logits = (t @ router).astype(jnp.float32)
    top_v, top_i = lax.top_k(logits, TOP_K)
    top_w = jax.nn.softmax(top_v, axis=-1)
    combine = jnp.sum(jax.nn.one_hot(top_i, N_EXPERTS, dtype=jnp.float32) * top_w[..., None], axis=1)
    out = jnp.zeros((b * s, d), jnp.float32)
    for e in range(N_EXPERTS):
        out = out + combine[:, e:e + 1] * swiglu(t, eg[e], eu[e], ed[e]).astype(jnp.float32)
    return out.astype(h.dtype).reshape(b, s, d)


def setup_inputs(seed: int = 0) -> dict:
    key = jax.random.key(seed)
    ks = jax.random.split(key, 20)
    n_dense = (DEPTH + 1) // 2
    n_moe = DEPTH // 2
    nrm = jax.random.normal
    f32 = jnp.float32
    return {
        "x": nrm(ks[0], (BATCH, SEQ, D_MODEL), f32),
        "mix_norm": 1.0 + 0.02 * nrm(ks[1], (DEPTH, D_MODEL), f32),
        "w_in": nrm(ks[2], (DEPTH, D_MODEL, IN_TOTAL), f32) * D_MODEL ** -0.5,
        "w_gate_lr": nrm(ks[3], (DEPTH, GLA_GATE_RANK, GLA_KEY_WIDTH), f32) * GLA_GATE_RANK ** -0.5,
        "b_gate_lr": 0.1 * nrm(ks[4], (DEPTH, GLA_KEY_WIDTH), f32),
        "conv_w": nrm(ks[5], (DEPTH, CONV_K, CONV_WIDTH), f32) * CONV_K ** -0.5,
        "gla_norm": 1.0 + 0.02 * nrm(ks[6], (DEPTH, GLA_VAL_WIDTH), f32),
        "w_out": nrm(ks[7], (DEPTH, MIX_WIDTH, D_MODEL), f32) * MIX_WIDTH ** -0.5,
        "ffn_norm": 1.0 + 0.02 * nrm(ks[8], (DEPTH, D_MODEL), f32),
        "dense_w_gate": nrm(ks[9], (n_dense, D_MODEL, D_FF_DENSE), f32) * D_MODEL ** -0.5,
        "dense_w_up": nrm(ks[10], (n_dense, D_MODEL, D_FF_DENSE), f32) * D_MODEL ** -0.5,
        "dense_w_down": nrm(ks[11], (n_dense, D_FF_DENSE, D_MODEL), f32) * D_FF_DENSE ** -0.5,
        "router": nrm(ks[12], (n_moe, D_MODEL, N_EXPERTS), f32) * D_MODEL ** -0.5,
        "exp_w_gate": nrm(ks[13], (n_moe, N_EXPERTS, D_MODEL, D_FF_EXPERT), f32) * D_MODEL ** -0.5,
        "exp_w_up": nrm(ks[14], (n_moe, N_EXPERTS, D_MODEL, D_FF_EXPERT), f32) * D_MODEL ** -0.5,
        "exp_w_down": nrm(ks[15], (n_moe, N_EXPERTS, D_FF_EXPERT, D_MODEL), f32) * D_FF_EXPERT ** -0.5,
        "final_norm": 1.0 + 0.02 * nrm(ks[16], (D_MODEL,), f32),
    }


def reference(x, mix_norm, w_in, w_gate_lr, b_gate_lr, conv_w, gla_norm, w_out, ffn_norm,
              dense_w_gate, dense_w_up, dense_w_down, router, exp_w_gate, exp_w_up, exp_w_down,
              final_norm):
    cos, sin = rope_tables(x.shape[1], HEAD_DIM)
    h = x
    for i in range(DEPTH):
        hn = rmsnorm(h, mix_norm[i])
        h = h + hybrid_mixer(hn, w_in[i], w_gate_lr[i], b_gate_lr[i], conv_w[i], gla_norm[i],
                             w_out[i], cos, sin)
        hn = rmsnorm(h, ffn_norm[i])
        j = i // 2
        if i % 2 == 0:
            h = h + swiglu(hn, dense_w_gate[j], dense_w_up[j], dense_w_down[j])
        else:
            h = h + moe_swiglu(hn, router[j], exp_w_gate[j], exp_w_up[j], exp_w_down[j])
    return rmsnorm(h, final_norm)
```

```python
import functools

import jax
import jax.numpy as jnp
import numpy as np
from jax import lax
from jax.experimental import pallas as pl
from jax.experimental.pallas import tpu as pltpu

F32 = jnp.float32
BF16 = jnp.bfloat16

HEAD_DIM = 64
DILATIONS = (1, 4, 16)
BAND = 128
CONV_K = 3
GLA_HEADS = 4
GLA_DK = 32
GLA_DV = 64
GLA_KEY_WIDTH = GLA_HEADS * GLA_DK
GLA_VAL_WIDTH = GLA_HEADS * GLA_DV
GLA_GATE_RANK = 16
GLA_GATE_TAU = 16.0
GLA_CHUNK = 32
N_EXPERTS = 8
ROPE_THETA = 10000.0
EPS = 1e-6
NEG = -1e30

LANES = 128
SUBLANES = 8
VMEM_BYTES_V7X = 64 * 1024 * 1024

TM_PROJ = 256
TQ_ATTN = 2048
TC_GLA = 512
TM_FFN = 256
TF_FFN = 256
TM_MOE = 512
TF_MOE = 896
TM_COMB = 512

HIGHEST = lax.Precision.HIGHEST


def _cparams(semantics, vmem_mb):
    assert vmem_mb * 1024 * 1024 < VMEM_BYTES_V7X
    return pltpu.CompilerParams(dimension_semantics=semantics,
                                vmem_limit_bytes=vmem_mb * 1024 * 1024)


def _rms(x, g):
    ms = jnp.mean(x * x, axis=-1, keepdims=True)
    return x * lax.rsqrt(ms + EPS) * g


def _silu(x):
    return x * (1.0 / (1.0 + jnp.exp(-x)))


def _resident(shape):
    nd = len(shape)
    return pl.BlockSpec(shape, lambda *_: (0,) * nd, pipeline_mode=pl.Buffered(1))


def _norm_proj_kernel(h_ref, g_ref, w_ref, cos_ref, sin_ref, qkv_ref, cv_ref, gl_ref, *, att_w, cv_w):
    xn = _rms(h_ref[...], g_ref[...]).astype(BF16)
    cos = cos_ref[...]
    sin = sin_ref[...]
    lane = lax.broadcasted_iota(jnp.int32, cos.shape, 1)
    first_half = (lane % HEAD_DIM) < (HEAD_DIM // 2)
    n_qk = 2 * att_w
    for c0 in range(0, 3 * att_w, att_w):
        y_all = jnp.dot(xn, w_ref[:, c0:c0 + att_w], preferred_element_type=F32)
        for c in range(c0, c0 + att_w, LANES):
            y = y_all[:, c - c0:c - c0 + LANES]
            if c < n_qk:
                partner = jnp.where(first_half,
                                    pltpu.roll(y, LANES - HEAD_DIM // 2, axis=1),
                                    pltpu.roll(y, HEAD_DIM // 2, axis=1))
                y = y * cos + partner * sin
                if c < att_w:
                    y = y * (HEAD_DIM ** -0.5)
            qkv_ref[:, c:c + LANES] = y
    base = 3 * att_w
    cv_ref[...] = jnp.dot(xn, w_ref[:, base:base + cv_w], preferred_element_type=F32)
    base += cv_w
    gl_ref[...] = jnp.dot(xn, w_ref[:, base:], preferred_element_type=F32)


def _norm_proj(h, g, w_pad, cos_t, sin_t, seq, att_w, cv_w, gl_w):
    t, d = h.shape
    tm = TM_PROJ
    n_s = seq // tm
    kern = functools.partial(_norm_proj_kernel, att_w=att_w, cv_w=cv_w)
    return pl.pallas_call(
        kern,
        grid=(t // tm,),
        in_specs=[
            pl.BlockSpec((tm, d), lambda i: (i, 0)),
            _resident((1, d)),
            _resident(w_pad.shape),
            pl.BlockSpec((tm, LANES), lambda i: (i % n_s, 0)),
            pl.BlockSpec((tm, LANES), lambda i: (i % n_s, 0)),
        ],
        out_specs=[
            pl.BlockSpec((tm, 3 * att_w), lambda i: (i, 0)),
            pl.BlockSpec((tm, cv_w), lambda i: (i, 0)),
            pl.BlockSpec((tm, gl_w), lambda i: (i, 0)),
        ],
        out_shape=[
            jax.ShapeDtypeStruct((t, 3 * att_w), F32),
            jax.ShapeDtypeStruct((t, cv_w), F32),
            jax.ShapeDtypeStruct((t, gl_w), F32),
        ],
        compiler_params=_cparams(("parallel",), 40),
        name="norm_proj",
    )(h, g, w_pad, cos_t, sin_t)


def _attn_kernel(q_ref, k_ref, v_ref, o_ref, m0_sc, m1_sc, l0_sc, l1_sc, acc_sc, *, tq):
    t = pl.program_id(2)
    shape = (BAND, LANES)
    m0_sc[...] = jnp.full(m0_sc.shape, NEG, F32)
    m1_sc[...] = jnp.full(m1_sc.shape, NEG, F32)
    l0_sc[...] = jnp.zeros(l0_sc.shape, F32)
    l1_sc[...] = jnp.zeros(l1_sc.shape, F32)
    acc_sc[...] = jnp.zeros(acc_sc.shape, F32)

    lane = lax.broadcasted_iota(jnp.int32, shape, 1)
    row = lax.broadcasted_iota(jnp.int32, shape, 0)
    head0 = lane < HEAD_DIM
    bias_prev = jnp.where(lane >= row, 0.0, NEG).astype(F32)
    bias_cur = jnp.where(lane <= row, 0.0, NEG).astype(F32)
    bias_none = jnp.full(shape, NEG, F32)
    nt = (((1,), (1,)), ((), ()))

    for d in DILATIONS:
        blocks_per_residue = (tq // BAND) // d

        def rows(start, d=d):
            return pl.ds(start, BAND) if d == 1 else pl.ds(start, BAND, stride=d)

        def unit(u, carry, d=d, blocks_per_residue=blocks_per_residue, rows=rows):
            r = u // blocks_per_residue
            j = u % blocks_per_residue
            q0 = j * (BAND * d) + r
            kc0 = t * tq + q0
            has_prev = kc0 >= BAND * d
            kp0 = jnp.where(has_prev, kc0 - BAND * d, kc0)
            q = q_ref[rows(q0), :]
            k_prev = k_ref[rows(kp0), :].astype(BF16)
            k_cur = k_ref[rows(kc0), :].astype(BF16)
            v_prev = v_ref[rows(kp0), :].astype(BF16)
            v_cur = v_ref[rows(kc0), :].astype(BF16)
            b_prev = jnp.where(has_prev, bias_prev, bias_none)
            acc_old = acc_sc[rows(q0), :]
            acc_new = []
            for h, (m_sc, l_sc) in enumerate(((m0_sc, l0_sc), (m1_sc, l1_sc))):
                qh = jnp.where(head0 if h == 0 else ~head0, q, 0.0).astype(BF16)
                s_prev = lax.dot_general(qh, k_prev, nt, preferred_element_type=F32) + b_prev
                s_cur = lax.dot_general(qh, k_cur, nt, preferred_element_type=F32) + bias_cur
                m_old = m_sc[rows(q0), :]
                l_old = l_sc[rows(q0), :]
                m_blk = jnp.max(jnp.maximum(s_prev, s_cur), axis=-1, keepdims=True)
                m_new = jnp.maximum(m_old, m_blk)
                p_prev = jnp.exp(s_prev - m_new)
                p_cur = jnp.exp(s_cur - m_new)
                alpha = jnp.exp(m_old - m_new)
                l_new = alpha * l_old + jnp.sum(p_prev + p_cur, axis=-1, keepdims=True)
                pv = (jnp.dot(p_prev.astype(BF16), v_prev, preferred_element_type=F32)
                      + jnp.dot(p_cur.astype(BF16), v_cur, preferred_element_type=F32))
                m_sc[rows(q0), :] = m_new
                l_sc[rows(q0), :] = l_new
                acc_new.append(alpha * acc_old + pv)
            acc_sc[rows(q0), :] = jnp.where(head0, acc_new[0], acc_new[1])
            return carry

        lax.fori_loop(0, tq // BAND, unit, 0)

    lane_t = lax.broadcasted_iota(jnp.int32, acc_sc.shape, 1)
    inv_l = jnp.where(lane_t < HEAD_DIM, 1.0 / l0_sc[...], 1.0 / l1_sc[...])
    o_ref[...] = (acc_sc[...] * inv_l).astype(o_ref.dtype)


def _dilated_attn(qkv, att_w):
    b, s, _ = qkv.shape
    tq = TQ_ATTN
    n_hp = att_w // LANES
    kern = functools.partial(_attn_kernel, tq=tq)
    return pl.pallas_call(
        kern,
        grid=(b, n_hp, s // tq),
        in_specs=[
            pl.BlockSpec((None, tq, LANES), lambda bi, hp, ti: (bi, ti, hp)),
            pl.BlockSpec((None, s, LANES), lambda bi, hp, ti: (bi, 0, n_hp + hp)),
            pl.BlockSpec((None, s, LANES), lambda bi, hp, ti: (bi, 0, 2 * n_hp + hp)),
        ],
        out_specs=pl.BlockSpec((None, tq, LANES), lambda bi, hp, ti: (bi, ti, hp)),
        out_shape=jax.ShapeDtypeStruct((b, s, att_w), BF16),
        scratch_shapes=[pltpu.VMEM((tq, LANES), F32)] * 5,
        compiler_params=_cparams(("parallel", "parallel", "arbitrary"), 48),
        name="dilated_attn",
    )(qkv, qkv, qkv)


def _conv_gla_kernel(cv_ref, cvp_ref, gl_ref, cw_ref, wlr_ref, blr_ref, gn_ref,
                     cv_out, gl_out, u_sc, a_sc, o_sc, st_sc, *, tc, cw):
    ti = pl.program_id(1)
    kw, vw = GLA_KEY_WIDTH, GLA_VAL_WIDTH

    gate_b = cv_ref[:, 0:cw]
    u = cv_ref[:, cw:2 * cw] * cv_ref[:, 2 * cw:3 * cw]
    u_halo = cvp_ref[:, cw:2 * cw] * cvp_ref[:, 2 * cw:3 * cw]
    u_sc[0:SUBLANES, :] = jnp.where(ti > 0, u_halo, 0.0)
    u_sc[SUBLANES:, :] = u
    y = (cw_ref[0:1, :] * u_sc[SUBLANES - 2:SUBLANES - 2 + tc, :]
         + cw_ref[1:2, :] * u_sc[SUBLANES - 1:SUBLANES - 1 + tc, :]
         + cw_ref[2:3, :] * u)
    cv_out[...] = (gate_b * y).astype(cv_out.dtype)

    @pl.when(ti == 0)
    def _():
        st_sc[...] = jnp.zeros(st_sc.shape, F32)

    c0 = 2 * kw + 2 * vw
    z = jnp.dot(gl_ref[:, c0:c0 + LANES], wlr_ref[...], precision=HIGHEST,
                preferred_element_type=F32) + blr_ref[...]
    a_sc[...] = (jnp.minimum(z, 0.0) - jnp.log1p(jnp.exp(-jnp.abs(z)))) * (1.0 / GLA_GATE_TAU)

    c = GLA_CHUNK
    tri = (lax.broadcasted_iota(jnp.int32, (c, c), 0)
           >= lax.broadcasted_iota(jnp.int32, (c, c), 1)).astype(F32)
    row_t = lax.broadcasted_iota(jnp.int32, (c, kw), 0)
    spread = (lax.broadcasted_iota(jnp.int32, (kw, vw), 0) // GLA_DK
              == lax.broadcasted_iota(jnp.int32, (kw, vw), 1) // GLA_DV).astype(BF16)
    state_mask = (lax.broadcasted_iota(jnp.int32, (vw, kw), 0) // GLA_DV
                  == lax.broadcasted_iota(jnp.int32, (vw, kw), 1) // GLA_DK).astype(F32)
    nt = (((1,), (1,)), ((), ()))

    def chunk(ci, carry):
        r0 = pl.multiple_of(ci * c, c)
        q = gl_ref[pl.ds(r0, c), 0:kw] * (GLA_DK ** -0.5)
        k = gl_ref[pl.ds(r0, c), kw:2 * kw]
        v = gl_ref[pl.ds(r0, c), 2 * kw:2 * kw + vw]
        a = a_sc[pl.ds(r0, c), :]
        bcum = jnp.dot(tri, a, precision=HIGHEST, preferred_element_type=F32)
        parts = []
        for s in range(c):
            dec = jnp.exp(jnp.minimum(bcum - bcum[s:s + 1, :], 0.0))
            parts.append(jnp.where(row_t >= s, q * k[s:s + 1, :] * dec, 0.0))
        p_all = jnp.concatenate(parts, axis=0).astype(BF16)
        e_all = jnp.dot(p_all, spread, preferred_element_type=F32)
        o = jnp.zeros((c, vw), F32)
        for s in range(c):
            o = o + e_all[s * c:(s + 1) * c, :] * v[s:s + 1, :]
        st = st_sc[...]
        qd = (q * jnp.exp(bcum)).astype(BF16)
        o = o + lax.dot_general(qd, st.astype(BF16), nt, preferred_element_type=F32)
        b_last = bcum[c - 1:c, :]
        k_end = (k * jnp.exp(b_last - bcum)).astype(BF16)
        upd = jnp.dot(v.T.astype(BF16), k_end, preferred_element_type=F32)
        st_sc[...] = st * jnp.exp(b_last) + upd * state_mask
        o_sc[pl.ds(r0, c), :] = o
        return carry

    lax.fori_loop(0, tc // c, chunk, 0)

    o = o_sc[...]
    group = (lax.broadcasted_iota(jnp.int32, (vw, vw), 0) // GLA_DV
             == lax.broadcasted_iota(jnp.int32, (vw, vw), 1) // GLA_DV).astype(F32) * (1.0 / GLA_DV)
    ms = jnp.dot(o * o, group, precision=HIGHEST, preferred_element_type=F32)
    gate = _silu(gl_ref[:, 2 * kw + vw:2 * kw + 2 * vw])
    gl_out[...] = (o * lax.rsqrt(ms + EPS) * gn_ref[...] * gate).astype(gl_out.dtype)


def _conv_gla(cvin, glin, conv_w, wlr_pad, blr, gnorm, batch, seq):
    t, cv3 = cvin.shape
    cw = cv3 // 3
    gw = glin.shape[1]
    tc = TC_GLA
    n_t = seq // tc
    halo_blocks = tc // SUBLANES
    kern = functools.partial(_conv_gla_kernel, tc=tc, cw=cw)
    cvin3 = cvin.reshape(batch, seq, cv3)
    glin3 = glin.reshape(batch, seq, gw)
    cv_o, gl_o = pl.pallas_call(
        kern,
        grid=(batch, n_t),
        in_specs=[
            pl.BlockSpec((None, tc, cv3), lambda bi, ti: (bi, ti, 0)),
            pl.BlockSpec((None, SUBLANES, cv3),
                         lambda bi, ti: (bi, jnp.maximum(ti * halo_blocks - 1, 0), 0)),
            pl.BlockSpec((None, tc, gw), lambda bi, ti: (bi, ti, 0)),
            _resident(conv_w.shape),
            _resident(wlr_pad.shape),
            _resident(blr.shape),
            _resident(gnorm.shape),
        ],
        out_specs=[
            pl.BlockSpec((None, tc, cw), lambda bi, ti: (bi, ti, 0)),
            pl.BlockSpec((None, tc, GLA_VAL_WIDTH), lambda bi, ti: (bi, ti, 0)),
        ],
        out_shape=[
            jax.ShapeDtypeStruct((batch, seq, cw), BF16),
            jax.ShapeDtypeStruct((batch, seq, GLA_VAL_WIDTH), BF16),
        ],
        scratch_shapes=[
            pltpu.VMEM((tc + SUBLANES, cw), F32),
            pltpu.VMEM((tc, GLA_KEY_WIDTH), F32),
            pltpu.VMEM((tc, GLA_VAL_WIDTH), F32),
            pltpu.VMEM((GLA_VAL_WIDTH, GLA_KEY_WIDTH), F32),
        ],
        compiler_params=_cparams(("parallel", "arbitrary"), 40),
        name="conv_gla",
    )(cvin3, cvin3, glin3, conv_w, wlr_pad, blr, gnorm)
    return cv_o.reshape(t, cw), gl_o.reshape(t, GLA_VAL_WIDTH)


def _out_proj(h_ref, att_ref, cv_ref, gl_ref, wo_ref):
    aw = att_ref.shape[1]
    cw = cv_ref.shape[1]
    return (h_ref[...]
            + jnp.dot(att_ref[...], wo_ref[0:aw, :], preferred_element_type=F32)
            + jnp.dot(cv_ref[...], wo_ref[aw:aw + cw, :], preferred_element_type=F32)
            + jnp.dot(gl_ref[...], wo_ref[aw + cw:, :], preferred_element_type=F32))


def _out_ffn_kernel(h_ref, att_ref, cv_ref, gl_ref, wo_ref, g_ref, wg_ref, wu_ref, wd_ref,
                    o_ref, hn_sc, acc_sc):
    h1 = _out_proj(h_ref, att_ref, cv_ref, gl_ref, wo_ref)
    hn_sc[...] = _rms(h1, g_ref[...]).astype(BF16)
    acc_sc[...] = h1

    def chunk(f, carry):
        hn = hn_sc[...]
        gate = jnp.dot(hn, wg_ref[f], preferred_element_type=F32)
        up = jnp.dot(hn, wu_ref[f], preferred_element_type=F32)
        act = (_silu(gate) * up).astype(BF16)
        acc_sc[...] += jnp.dot(act, wd_ref[f], preferred_element_type=F32)
        return carry

    lax.fori_loop(0, wg_ref.shape[0], chunk, 0)
    o_ref[...] = acc_sc[...]


def _out_ffn(h, att, cv, gl, wo, g, wg3, wu3, wd3):
    t, d = h.shape
    tm = TM_FFN
    row = lambda w: pl.BlockSpec((tm, w), lambda i: (i, 0))
    return pl.pallas_call(
        _out_ffn_kernel,
        grid=(t // tm,),
        in_specs=[row(d), row(att.shape[1]), row(cv.shape[1]), row(gl.shape[1]),
                  _resident(wo.shape), _resident(g.shape),
                  _resident(wg3.shape), _resident(wu3.shape), _resident(wd3.shape)],
        out_specs=row(d),
        out_shape=jax.ShapeDtypeStruct((t, d), F32),
        scratch_shapes=[pltpu.VMEM((tm, d), BF16), pltpu.VMEM((tm, d), F32)],
        compiler_params=_cparams(("parallel",), 48),
        name="out_ffn",
    )(h, att, cv, gl, wo, g, wg3, wu3, wd3)


def _out_route_kernel(h_ref, att_ref, cv_ref, gl_ref, wo_ref, g_ref, wr_ref,
                      h1_ref, hn_ref, route_ref):
    h1 = _out_proj(h_ref, att_ref, cv_ref, gl_ref, wo_ref)
    hn = _rms(h1, g_ref[...])
    h1_ref[...] = h1
    hn_ref[...] = hn.astype(BF16)
    logits = jnp.dot(hn, wr_ref[...], precision=HIGHEST, preferred_element_type=F32)
    lane = lax.broadcasted_iota(jnp.int32, logits.shape, 1)
    logits = jnp.where(lane < N_EXPERTS, logits, -jnp.inf)
    v1 = jnp.max(logits, axis=-1, keepdims=True)
    i1 = jnp.min(jnp.where(logits == v1, lane, LANES), axis=-1, keepdims=True)
    rest = jnp.where(lane == i1, -jnp.inf, logits)
    v2 = jnp.max(rest, axis=-1, keepdims=True)
    i2 = jnp.min(jnp.where(rest == v2, lane, LANES), axis=-1, keepdims=True)
    e2 = jnp.exp(v2 - v1)
    w1 = 1.0 / (1.0 + e2)
    w2 = e2 / (1.0 + e2)
    route_ref[...] = jnp.where(lane == 0, i1.astype(F32),
                               jnp.where(lane == 1, i2.astype(F32),
                                         jnp.where(lane == 2, w1, jnp.where(lane == 3, w2, 0.0))))


def _out_route(h, att, cv, gl, wo, g, wr_pad):
    t, d = h.shape
    tm = TM_PROJ
    row = lambda w: pl.BlockSpec((tm, w), lambda i: (i, 0))
    return pl.pallas_call(
        _out_route_kernel,
        grid=(t // tm,),
        in_specs=[row(d), row(att.shape[1]), row(cv.shape[1]), row(gl.shape[1]),
                  _resident(wo.shape), _resident(g.shape), _resident(wr_pad.shape)],
        out_specs=[row(d), row(d), row(LANES)],
        out_shape=[jax.ShapeDtypeStruct((t, d), F32),
                   jax.ShapeDtypeStruct((t, d), BF16),
                   jax.ShapeDtypeStruct((t, LANES), F32)],
        compiler_params=_cparams(("parallel",), 32),
        name="out_route",
    )(h, att, cv, gl, wo, g, wr_pad)


def _expert_ffn_kernel(te_ref, tv_ref, x_ref, w_ref, wg_ref, wu_ref, wd_ref, y_ref, acc_sc):
    i = pl.program_id(0)
    f = pl.program_id(1)

    @pl.when(f == 0)
    def _():
        acc_sc[...] = jnp.zeros(acc_sc.shape, F32)

    @pl.when(tv_ref[i] > 0)
    def _():
        x = x_ref[...]
        gate = jnp.dot(x, wg_ref[...], preferred_element_type=F32)
        up = jnp.dot(x, wu_ref[...], preferred_element_type=F32)
        act = (_silu(gate) * up).astype(BF16)
        acc_sc[...] += jnp.dot(act, wd_ref[...], preferred_element_type=F32)

    @pl.when(f == pl.num_programs(1) - 1)
    def _():
        y_ref[...] = acc_sc[...] * w_ref[...]


def _expert_ffn(tile_expert, tile_valid, xs, wrow, eg, eu, ed):
    r, d = xs.shape
    tm, tf = TM_MOE, TF_MOE
    ff = eg.shape[2]
    n_f = ff // tf
    last = n_f - 1

    def fi(i, f, tv):
        return jnp.where(tv[i] > 0, f, last)

    grid_spec = pltpu.PrefetchScalarGridSpec(
        num_scalar_prefetch=2,
        grid=(r // tm, n_f),
        in_specs=[
            pl.BlockSpec((tm, d), lambda i, f, te, tv: (i, 0)),
            pl.BlockSpec((tm, 1), lambda i, f, te, tv: (i, 0)),
            pl.BlockSpec((None, d, tf), lambda i, f, te, tv: (te[i], 0, fi(i, f, tv))),
            pl.BlockSpec((None, d, tf), lambda i, f, te, tv: (te[i], 0, fi(i, f, tv))),
            pl.BlockSpec((None, tf, d), lambda i, f, te, tv: (te[i], fi(i, f, tv), 0)),
        ],
        out_specs=pl.BlockSpec((tm, d), lambda i, f, te, tv: (i, 0)),
        scratch_shapes=[pltpu.VMEM((tm, d), F32)],
    )
    return pl.pallas_call(
        _expert_ffn_kernel,
        grid_spec=grid_spec,
        out_shape=jax.ShapeDtypeStruct((r, d), F32),
        compiler_params=_cparams(("arbitrary", "arbitrary"), 48),
        name="expert_ffn",
    )(tile_expert, tile_valid, xs, wrow, eg, eu, ed)


def _combine_kernel(h_ref, y0_ref, y1_ref, g_ref, o_ref, *, final):
    h = h_ref[...] + (y0_ref[...] + y1_ref[...])
    o_ref[...] = _rms(h, g_ref[...]) if final else h


def _combine(h1, y0, y1, g, final):
    t, d = h1.shape
    tm = TM_COMB
    row = pl.BlockSpec((tm, d), lambda i: (i, 0))
    return pl.pallas_call(
        functools.partial(_combine_kernel, final=final),
        grid=(t // tm,),
        in_specs=[row, row, row, _resident(g.shape)],
        out_specs=row,
        out_shape=jax.ShapeDtypeStruct((t, d), F32),
        compiler_params=_cparams(("parallel",), 32),
        name="combine",
    )(h1, y0, y1, g)


def _final_norm_kernel(h_ref, g_ref, o_ref):
    o_ref[...] = _rms(h_ref[...], g_ref[...])


def _final_norm(h, g):
    t, d = h.shape
    tm = TM_COMB
    row = pl.BlockSpec((tm, d), lambda i: (i, 0))
    return pl.pallas_call(
        _final_norm_kernel,
        grid=(t // tm,),
        in_specs=[row, _resident(g.shape)],
        out_specs=row,
        out_shape=jax.ShapeDtypeStruct((t, d), F32),
        compiler_params=_cparams(("parallel",), 32),
        name="final_norm",
    )(h, g)


def _rope_tables(seq):
    inv = 1.0 / (ROPE_THETA ** (jnp.arange(0, HEAD_DIM, 2, dtype=F32) / HEAD_DIM))
    ang = jnp.arange(seq, dtype=F32)[:, None] * inv[None, :]
    cos, sin = jnp.cos(ang), jnp.sin(ang)
    reps = LANES // HEAD_DIM
    cos_t = jnp.tile(jnp.concatenate([cos, cos], axis=1), (1, reps))
    sin_t = jnp.tile(jnp.concatenate([-sin, sin], axis=1), (1, reps))
    return cos_t, sin_t


def _route_plan(top_i, top_w, tm):
    t = top_i.shape[0]
    n_tiles = (2 * t) // tm + N_EXPERTS
    e_flat = jnp.concatenate([top_i[:, 0], top_i[:, 1]])
    w_flat = jnp.concatenate([top_w[:, 0], top_w[:, 1]])
    onehot = (e_flat[:, None] == jnp.arange(N_EXPERTS, dtype=jnp.int32)[None, :]).astype(jnp.int32)
    csum = jnp.cumsum(onehot, axis=0)
    rank = jnp.sum(csum * onehot, axis=1) - 1
    counts = csum[-1]
    tiles_per = (counts + tm - 1) // tm
    tile_end = jnp.cumsum(tiles_per)
    pad_start = (tile_end - tiles_per) * tm
    dest = jnp.sum(onehot * pad_start[None, :], axis=1) + rank
    token = jnp.concatenate([jnp.arange(t, dtype=jnp.int32)] * 2)
    src = jnp.zeros((n_tiles * tm,), jnp.int32).at[dest].set(token)
    wrow = jnp.zeros((n_tiles * tm,), F32).at[dest].set(w_flat)
    tile_id = jnp.arange(n_tiles, dtype=jnp.int32)
    tile_valid = (tile_id < tile_end[-1]).astype(jnp.int32)
    tile_expert = jnp.sum((tile_id[:, None] >= tile_end[None, :]).astype(jnp.int32), axis=1)
    last_expert = jnp.sum((tile_end[-1] - 1 >= tile_end).astype(jnp.int32))
    tile_expert = jnp.where(tile_valid > 0, tile_expert, last_expert).astype(jnp.int32)
    return src, wrow[:, None], dest, tile_expert, tile_valid


def kernel(x, mix_norm, w_in, w_gate_lr, b_gate_lr, conv_w, gla_norm, w_out, ffn_norm,
           dense_w_gate, dense_w_up, dense_w_down, router, exp_w_gate, exp_w_up, exp_w_down,
           final_norm):
    batch, seq, d_model = x.shape
    depth = w_in.shape[0]
    t = batch * seq
    in_total = w_in.shape[2]
    cw = conv_w.shape[2]
    att_w = (in_total - 3 * cw - 2 * GLA_KEY_WIDTH - 2 * GLA_VAL_WIDTH - GLA_GATE_RANK) // 3
    gl_real = 2 * GLA_KEY_WIDTH + 2 * GLA_VAL_WIDTH + GLA_GATE_RANK
    gl_w = -(-gl_real // LANES) * LANES
    assert seq % TQ_ATTN == 0 and seq % TC_GLA == 0 and t % TM_MOE == 0
    assert att_w % LANES == 0 and cw % LANES == 0

    cos_t, sin_t = _rope_tables(seq)
    h = x.reshape(t, d_model)
    for i in range(depth):
        w_pad = jnp.pad(w_in[i], ((0, 0), (0, gl_w - gl_real))).astype(BF16)
        qkv, cvin, glin = _norm_proj(h, mix_norm[i][None, :], w_pad, cos_t, sin_t,
                                     seq, att_w, 3 * cw, gl_w)
        att = _dilated_attn(qkv.reshape(batch, seq, 3 * att_w), att_w).reshape(t, att_w)
        wlr_pad = jnp.pad(w_gate_lr[i], ((0, LANES - GLA_GATE_RANK), (0, 0)))
        cv, gl = _conv_gla(cvin, glin, conv_w[i], wlr_pad, b_gate_lr[i][None, :],
                           gla_norm[i][None, :], batch, seq)
        wo = w_out[i].astype(BF16)
        g2 = ffn_norm[i][None, :]
        j = i // 2
        last = i == depth - 1
        if i % 2 == 0:
            ff = dense_w_gate.shape[2]
            n_f = ff // TF_FFN
            wg3 = dense_w_gate[j].astype(BF16).reshape(d_model, n_f, TF_FFN).transpose(1, 0, 2)
            wu3 = dense_w_up[j].astype(BF16).reshape(d_model, n_f, TF_FFN).transpose(1, 0, 2)
            wd3 = dense_w_down[j].astype(BF16).reshape(n_f, TF_FFN, d_model)
            h = _out_ffn(h, att, cv, gl, wo, g2, wg3, wu3, wd3)
            if last:
                h = _final_norm(h, final_norm[None, :])
        else:
            wr_pad = jnp.pad(router[j], ((0, 0), (0, LANES - N_EXPERTS)))
            h1, hn, route = _out_route(h, att, cv, gl, wo, g2, wr_pad)
            top_i = route[:, 0:2].astype(jnp.int32)
            top_w = route[:, 2:4]
            src, wrow, dest, tile_expert, tile_valid = _route_plan(top_i, top_w, TM_MOE)
            xs = jnp.take(hn, src, axis=0)
            y = _expert_ffn(tile_expert, tile_valid, xs, wrow,
                            exp_w_gate[j].astype(BF16), exp_w_up[j].astype(BF16),
                            exp_w_down[j].astype(BF16))
            y0 = jnp.take(y, dest[:t], axis=0)
            y1 = jnp.take(y, dest[t:], axis=0)
            h = _combine(h1, y0, y1, final_norm[None, :] if last else jnp.ones((1, d_model), F32), last)
    return h.reshape(batch, seq, d_model)
```
